```python
import jax, jax.numpy as jnp
from jax import lax
import numpy as np

D_MODEL = 1024
BATCH = 4
SEQ = 8192
DEPTH = 1

N_MEM = 256
D_MIX = D_MODEL
D_GLA = D_MIX // 2
GLA_HEADS = 4
GLA_DK = (D_GLA // 2) // GLA_HEADS
GLA_DV = D_GLA // GLA_HEADS
GLA_RANK = 16
GLA_NORMALIZER = 16.0
CHUNK = 64
D_POOL = D_MIX // 4
POOL_WINDOWS = (2, 4, 8, 16)
POOL_GROUP = D_POOL // len(POOL_WINDOWS)
D_MEM = D_MIX - D_GLA - D_POOL
MEM_HEADS = 4
MEM_DH = D_MEM // MEM_HEADS
EPS = 1e-6

SPLIT_SIZES = (
    GLA_HEADS * GLA_DK,
    GLA_HEADS * GLA_DK,
    D_GLA,
    GLA_RANK,
    D_GLA,
    D_POOL,
    D_POOL,
    D_MEM,
    D_MEM,
)
D_IN = int(sum(SPLIT_SIZES))
SPLIT_IDX = [int(i) for i in np.cumsum(SPLIT_SIZES)[:-1]]

kernel_name = "hybrid_gla_pool_memxattn_block"


def rmsnorm(x, g):
    x32 = x.astype(jnp.float32)
    y = x32 * lax.rsqrt(jnp.mean(x32 * x32, axis=-1, keepdims=True) + EPS)
    return (y * g.astype(jnp.float32)).astype(x.dtype)


def gla_chunked(q, k, v, gk):
    B, T, H, dk = q.shape
    dv = v.shape[-1]
    n = T // CHUNK

    def to_chunks(a):
        return a.astype(jnp.float32).reshape(B, n, CHUNK, H, a.shape[-1]).transpose(0, 3, 1, 2, 4)

    q, k, v, gk = to_chunks(q) * (dk ** -0.5), to_chunks(k), to_chunks(v), to_chunks(gk)
    b = jnp.cumsum(gk, axis=3)
    q_d = q * jnp.exp(b)
    k_d = k * jnp.exp(-b)
    scores = jnp.einsum('bhnid,bhnjd->bhnij', q_d, k_d)
    tril = jnp.tril(jnp.ones((CHUNK, CHUNK), dtype=bool))
    scores = jnp.where(tril, scores, 0.0)
    o_intra = jnp.einsum('bhnij,bhnje->bhnie', scores, v)
    b_last = b[:, :, :, -1:, :]
    k_end = k * jnp.exp(b_last - b)
    chunk_state = jnp.einsum('bhncd,bhnce->bhnde', k_end, v)
    chunk_decay = jnp.exp(b_last[:, :, :, 0, :])

    def step(S, inp):
        dec, cs = inp
        return dec[..., None] * S + cs, S

    S0 = jnp.zeros((B, H, dk, dv), jnp.float32)
    _, S_prev = lax.scan(step, S0, (jnp.moveaxis(chunk_decay, 2, 0), jnp.moveaxis(chunk_state, 2, 0)))
    S_prev = jnp.moveaxis(S_prev, 0, 2)
    o_inter = jnp.einsum('bhncd,bhnde->bhnce', q_d, S_prev)
    o = o_intra + o_inter
    return o.transpose(0, 2, 3, 1, 4).reshape(B, T, H, dv)


def multiscale_pool(u):
    B, T, _ = u.shape
    u32 = u.astype(jnp.float32)
    cs = jnp.cumsum(u32, axis=1)
    W = max(POOL_WINDOWS)
    cs_pad = jnp.concatenate([jnp.zeros((B, W, D_POOL), jnp.float32), cs], axis=1)
    pos = jnp.arange(T)
    outs = []
    for g, w in enumerate(POOL_WINDOWS):
        lo, hi = g * POOL_GROUP, (g + 1) * POOL_GROUP
        c_now = cs[:, :, lo:hi]
        c_prev = cs_pad[:, W - w:W - w + T, lo:hi]
        count = jnp.minimum(pos + 1, w).astype(jnp.float32)[None, :, None]
        outs.append((c_now - c_prev) / count - u32[:, :, lo:hi])
    return jnp.stack(outs, axis=2)


def memory_attention(q_mem, mem_n, w_mem_kv):
    B, T, _ = q_mem.shape
    M = mem_n.shape[1]
    kv = mem_n @ w_mem_kv
    k_m, v_m = jnp.split(kv, 2, axis=-1)
    q = q_mem.reshape(B, T, MEM_HEADS, MEM_DH)
    k_m = k_m.reshape(B, M, MEM_HEADS, MEM_DH)
    v_m = v_m.reshape(B, M, MEM_HEADS, MEM_DH)
    s = jnp.einsum('bthd,bmhd->bhtm', q.astype(jnp.float32), k_m.astype(jnp.float32)) * (MEM_DH ** -0.5)
    p = jax.nn.softmax(s, axis=-1)
    o = jnp.einsum('bhtm,bmhd->bthd', p, v_m.astype(jnp.float32))
    return o.reshape(B, T, D_MEM)


def setup_inputs(seed: int = 0) -> dict:
    key = jax.random.key(seed)
    ks = jax.random.split(key, 14)
    f = jnp.float32
    return {
        "x": jax.random.normal(ks[0], (BATCH, SEQ, D_MODEL), f),
        "mem": jax.random.normal(ks[1], (BATCH, N_MEM, D_MODEL), f),
        "norm_g": 1.0 + 0.02 * jax.random.normal(ks[2], (D_MODEL,), f),
        "w_in": jax.random.normal(ks[3], (D_MODEL, D_IN), f) * D_MODEL ** -0.5,
        "w_gk_up": jax.random.normal(ks[4], (GLA_RANK, GLA_HEADS * GLA_DK), f) * GLA_RANK ** -0.5,
        "b_gk": 0.01 * jax.random.normal(ks[5], (GLA_HEADS * GLA_DK,), f),
        "gla_norm_g": 1.0 + 0.02 * jax.random.normal(ks[6], (GLA_DV,), f),
        "pool_w": jax.random.normal(ks[7], (len(POOL_WINDOWS), POOL_GROUP, POOL_GROUP), f) * POOL_GROUP ** -0.5,
        "pool_scale": 1.0 + 0.02 * jax.random.normal(ks[8], (D_POOL,), f),
        "mem_norm_g": 1.0 + 0.02 * jax.random.normal(ks[9], (D_MODEL,), f),
        "w_mem_kv": jax.random.normal(ks[10], (D_MODEL, 2 * D_MEM), f) * D_MODEL ** -0.5,
        "w_out": jax.random.normal(ks[11], (D_MIX, D_MODEL), f) * D_MIX ** -0.5,
        "final_g": 1.0 + 0.02 * jax.random.normal(ks[12], (D_MODEL,), f),
    }


def reference(x, mem, norm_g, w_in, w_gk_up, b_gk, gla_norm_g, pool_w, pool_scale,
              mem_norm_g, w_mem_kv, w_out, final_g):
    B, T, _ = x.shape
    dt = x.dtype
    mem_n = rmsnorm(mem, mem_norm_g)
    for _layer in range(DEPTH):
        h = rmsnorm(x, norm_g)
        proj = h @ w_in
        q_g, k_g, v_g, gk_lr, gate_g, u_p, gate_p, q_m, gate_m = jnp.split(proj, SPLIT_IDX, axis=-1)

        gk_logit = (gk_lr @ w_gk_up + b_gk).astype(jnp.float32)
        gk = jax.nn.log_sigmoid(gk_logit) / GLA_NORMALIZER
        o_gla = gla_chunked(q_g.reshape(B, T, GLA_HEADS, GLA_DK),
                            k_g.reshape(B, T, GLA_HEADS, GLA_DK),
                            v_g.reshape(B, T, GLA_HEADS, GLA_DV),
                            gk.reshape(B, T, GLA_HEADS, GLA_DK))
        o_gla = rmsnorm(o_gla, gla_norm_g).reshape(B, T, D_GLA)
        o_gla = (o_gla * jax.nn.silu(gate_g.astype(jnp.float32))).astype(dt)

        pooled = multiscale_pool(u_p)
        o_pool = jnp.einsum('btgp,gpq->btgq', pooled, pool_w.astype(jnp.float32)).reshape(B, T, D_POOL)
        o_pool = (o_pool * pool_scale.astype(jnp.float32) * jax.nn.silu(gate_p.astype(jnp.float32))).astype(dt)

        o_mem = memory_attention(q_m, mem_n, w_mem_kv)
        o_mem = (o_mem * jax.nn.silu(gate_m.astype(jnp.float32))).astype(dt)

        mixed = jnp.concatenate([o_gla, o_pool, o_mem], axis=-1)
        x = x + mixed @ w_out
    return rmsnorm(x, final_g)
```

```python
import functools

import jax
import jax.numpy as jnp
from jax import lax
from jax.experimental import pallas as pl
from jax.experimental.pallas import tpu as pltpu

D_MODEL = 1024
N_MEM = 256
D_GLA = 512
GLA_HEADS = 4
GLA_DK = 64
GLA_DV = 128
D_QK = GLA_HEADS * GLA_DK
GLA_RANK = 16
GLA_NORMALIZER = 16.0
D_POOL = 256
POOL_WINDOWS = (2, 4, 8, 16)
POOL_GROUP = 64
POOL_HIST = max(POOL_WINDOWS)
D_MEM = 256
MEM_HEADS = 4
MEM_DH = 64
EPS = 1e-6

LANES = 128
TB = 512
CHUNK = 128
VMEM_LIMIT_BYTES = 56 * 1024 * 1024

C_Q, C_K, C_V, C_GG, C_U, C_GP, C_QM, C_GM = 0, 256, 512, 1024, 1536, 1792, 2048, 2304
D_IN_MAIN = 2560

F32 = jnp.float32
BF16 = jnp.bfloat16


def _dot(a, b):
    return jnp.dot(a, b, preferred_element_type=F32)


def _dot_nt(a, b):
    return lax.dot_general(a, b, (((1,), (1,)), ((), ())), preferred_element_type=F32)


def _rms(x, g):
    return x * lax.rsqrt(jnp.mean(x * x, axis=-1, keepdims=True) + EPS) * g


def _silu(x):
    return x / (1.0 + jnp.exp(-x))


def _mem_kv_kernel(mem_ref, g_ref, w_ref, kbdt_ref, vbd_ref):
    mem_n = _rms(mem_ref[0], g_ref[...]).astype(BF16)
    kv = _dot(mem_n, w_ref[...])
    k_m = kv[:, :D_MEM] * (MEM_DH ** -0.5)
    v_m = kv[:, D_MEM:]
    lane = lax.broadcasted_iota(jnp.int32, (N_MEM, D_MEM), 1)
    for h in range(MEM_HEADS):
        in_head = (lane >= h * MEM_DH) & (lane < (h + 1) * MEM_DH)
        rows = slice(h * N_MEM, (h + 1) * N_MEM)
        kbdt_ref[0, rows, :] = jnp.where(in_head, k_m, 0.0).astype(BF16)
        vbd_ref[0, rows, :] = jnp.where(in_head, v_m, 0.0).astype(BF16)


def _block_kernel(x_ref, w_in_ref, w_gklr_ref, w_gkup_ref, b_gk_ref, norm_g_ref, gla_g_ref,
                  pool_w_ref, pool_scale_ref, kbdt_ref, vbd_ref, w_out_ref, final_g_ref,
                  o_ref, s_ref, sbd_ref, uext_ref, mixed_ref):
    t = pl.program_id(1)

    @pl.when(t == 0)
    def _reset():
        s_ref[...] = jnp.zeros_like(s_ref)
        sbd_ref[...] = jnp.zeros_like(sbd_ref)
        uext_ref[0:POOL_HIST, :] = jnp.zeros((POOL_HIST, D_POOL), F32)

    x = x_ref[0]
    h = _rms(x, norm_g_ref[...]).astype(BF16)

    qk = _dot(h, w_in_ref[:, C_Q:C_V])
    v_bf = _dot(h, w_in_ref[:, C_V:C_GG]).astype(BF16)
    gate_g = _dot(h, w_in_ref[:, C_GG:C_U])
    gk_lr = _dot(h, w_gklr_ref[...]).astype(BF16)
    logit = _dot(gk_lr, w_gkup_ref[...]) + b_gk_ref[...]
    gk = -(jnp.maximum(-logit, 0.0) + jnp.log(1.0 + jnp.exp(-jnp.abs(logit)))) * (1.0 / GLA_NORMALIZER)

    row_c = lax.broadcasted_iota(jnp.int32, (CHUNK, 2 * CHUNK), 0)
    col_c = lax.broadcasted_iota(jnp.int32, (CHUNK, 2 * CHUNK), 1)
    causal2 = (col_c & (CHUNK - 1)) <= row_c
    tril2 = jnp.where(causal2, 1.0, 0.0).astype(BF16)
    lane_p = lax.broadcasted_iota(jnp.int32, (CHUNK, LANES), 1)
    lo_half = lane_p < GLA_DK
    zeros_v = jnp.zeros((CHUNK, GLA_DV), BF16)
    gla_g = gla_g_ref[...]

    for c in range(TB // CHUNK):
        r = slice(c * CHUNK, (c + 1) * CHUNK)
        gk_c = gk[r]
        gk_hi = gk_c.astype(BF16)
        gk_lo = (gk_c - gk_hi.astype(F32)).astype(BF16)
        b = _dot(tril2, jnp.concatenate([gk_hi, gk_lo], axis=0))
        b_mid = b[CHUNK // 2 - 1:CHUNK // 2, :]
        b_last = b[CHUNK - 1:CHUNK, :]
        q_c = qk[r, 0:D_QK] * (GLA_DK ** -0.5)
        k_c = qk[r, D_QK:2 * D_QK]
        q_in = q_c * jnp.exp(b - b_mid)
        k_in = k_c * jnp.exp(b_mid - b)
        q_dec = (q_c * jnp.exp(b)).astype(BF16)
        k_end = k_c * jnp.exp(b_last - b)
        v_c = v_bf[r]

        o_pairs = []
        for p in range(GLA_HEADS // 2):
            lanes = slice(p * LANES, (p + 1) * LANES)
            q_p = q_in[:, lanes].astype(BF16)
            k_p = k_in[:, lanes]
            wt = jnp.concatenate([jnp.where(lo_half, k_p, 0.0), jnp.where(lo_half, 0.0, k_p)],
                                 axis=0).astype(BF16)
            sc = jnp.where(causal2, _dot_nt(q_p, wt), 0.0).astype(BF16)
            va = v_c[:, (2 * p) * GLA_DV:(2 * p + 1) * GLA_DV]
            vb = v_c[:, (2 * p + 1) * GLA_DV:(2 * p + 2) * GLA_DV]
            vbd = jnp.concatenate([jnp.concatenate([va, zeros_v], axis=1),
                                   jnp.concatenate([zeros_v, vb], axis=1)], axis=0)
            lhs = jnp.concatenate([sc, q_dec], axis=1)
            rhs = jnp.concatenate([vbd, sbd_ref[:, p * 2 * GLA_DV:(p + 1) * 2 * GLA_DV]], axis=0)
            o_pairs.append(_dot(lhs, rhs))

        upd = _dot(k_end.T.astype(BF16), v_c)
        dec_t = jnp.broadcast_to(jnp.exp(b_last), (CHUNK, D_QK)).T
        for hd in range(GLA_HEADS):
            rows = slice(hd * GLA_DK, (hd + 1) * GLA_DK)
            cols = slice(hd * GLA_DV, (hd + 1) * GLA_DV)
            s_new = s_ref[rows, :] * dec_t[rows, :] + upd[rows, cols]
            s_ref[rows, :] = s_new
            sbd_ref[rows, cols] = s_new.astype(BF16)

        o = jnp.concatenate(o_pairs, axis=1)
        o_n = jnp.concatenate(
            [_rms(o[:, hd * GLA_DV:(hd + 1) * GLA_DV], gla_g) for hd in range(GLA_HEADS)], axis=1)
        mixed_ref[r, 0:D_GLA] = (o_n * _silu(gate_g[r])).astype(BF16)

    up = _dot(h, w_in_ref[:, C_U:C_QM])
    u = up[:, :D_POOL]
    gate_p = up[:, D_POOL:]
    uext_ref[POOL_HIST:POOL_HIST + TB, :] = u

    def back(j, lanes):
        return uext_ref[POOL_HIST - j:POOL_HIST - j + TB, lanes]

    def window_sum(j0, n, lanes):
        if n == 1:
            return back(j0, lanes)
        return window_sum(j0, n // 2, lanes) + window_sum(j0 + n // 2, n // 2, lanes)

    pos = t * TB + lax.broadcasted_iota(jnp.int32, (TB, LANES), 0)
    lane_t = lax.broadcasted_iota(jnp.int32, (TB, LANES), 1)
    first_group = lane_t < POOL_GROUP
    pooled = []
    for tile in range(D_POOL // LANES):
        lanes = slice(tile * LANES, (tile + 1) * LANES)
        w_a, w_b = POOL_WINDOWS[2 * tile], POOL_WINDOWS[2 * tile + 1]
        s_a = window_sum(0, w_a, lanes)
        s_b = s_a + window_sum(w_a, w_b - w_a, lanes)
        width = jnp.where(first_group, w_a, w_b)
        count = jnp.minimum(pos + 1, width).astype(F32)
        pooled.append(jnp.where(first_group, s_a, s_b) / count - u[:, lanes])
    pooled = jnp.concatenate(pooled, axis=1).astype(BF16)
    o_pool = _dot(pooled, pool_w_ref[...]) * pool_scale_ref[...] * _silu(gate_p)
    mixed_ref[:, D_GLA:D_GLA + D_POOL] = o_pool.astype(BF16)
    uext_ref[0:POOL_HIST, :] = uext_ref[TB:TB + POOL_HIST, :]

    qm = _dot(h, w_in_ref[:, C_QM:D_IN_MAIN])
    gate_m = qm[:, D_MEM:]
    s = _dot_nt(qm[:, :D_MEM].astype(BF16), kbdt_ref[0])
    probs = []
    for hd in range(MEM_HEADS):
        s_h = s[:, hd * N_MEM:(hd + 1) * N_MEM]
        e = jnp.exp(s_h - jnp.max(s_h, axis=-1, keepdims=True))
        probs.append((e / jnp.sum(e, axis=-1, keepdims=True)).astype(BF16))
    o_mem = _dot(jnp.concatenate(probs, axis=1), vbd_ref[0])
    mixed_ref[:, D_GLA + D_POOL:] = (o_mem * _silu(gate_m)).astype(BF16)

    y = x + _dot(mixed_ref[...], w_out_ref[...])
    o_ref[0] = _rms(y, final_g_ref[...])


def _full(shape):
    return pl.BlockSpec(shape, lambda *_: (0,) * len(shape))


@jax.jit
def kernel(x, mem, norm_g, w_in, w_gk_up, b_gk, gla_norm_g, pool_w, pool_scale, mem_norm_g,
           w_mem_kv, w_out, final_g):
    B, T, D = x.shape
    assert D == D_MODEL and T % TB == 0 and mem.shape == (B, N_MEM, D_MODEL)

    gk0 = C_GG
    w_main = jnp.concatenate([w_in[:, :gk0], w_in[:, gk0 + GLA_RANK:]], axis=1).astype(BF16)
    w_gklr = jnp.pad(w_in[:, gk0:gk0 + GLA_RANK], ((0, 0), (0, LANES - GLA_RANK))).astype(BF16)
    w_gkup = jnp.pad(w_gk_up, ((0, LANES - GLA_RANK), (0, 0))).astype(BF16)
    pool_w_bd = jax.scipy.linalg.block_diag(*[pool_w[g] for g in range(len(POOL_WINDOWS))]).astype(BF16)
    row = lambda a: a.reshape(1, -1).astype(F32)

    kbdt, vbd = pl.pallas_call(
        _mem_kv_kernel,
        grid=(B,),
        in_specs=[pl.BlockSpec((1, N_MEM, D_MODEL), lambda b: (b, 0, 0)),
                  _full((1, D_MODEL)), _full((D_MODEL, 2 * D_MEM))],
        out_specs=[pl.BlockSpec((1, MEM_HEADS * N_MEM, D_MEM), lambda b: (b, 0, 0))] * 2,
        out_shape=[jax.ShapeDtypeStruct((B, MEM_HEADS * N_MEM, D_MEM), BF16)] * 2,
        name="mem_kv",
    )(mem, row(mem_norm_g), w_mem_kv.astype(BF16))

    return pl.pallas_call(
        _block_kernel,
        grid=(B, T // TB),
        in_specs=[pl.BlockSpec((1, TB, D_MODEL), lambda b, t: (b, t, 0)),
                  _full((D_MODEL, D_IN_MAIN)), _full((D_MODEL, LANES)), _full((LANES, D_QK)),
                  _full((1, D_QK)), _full((1, D_MODEL)), _full((1, GLA_DV)),
                  _full((D_POOL, D_POOL)), _full((1, D_POOL)),
                  pl.BlockSpec((1, MEM_HEADS * N_MEM, D_MEM), lambda b, t: (b, 0, 0)),
                  pl.BlockSpec((1, MEM_HEADS * N_MEM, D_MEM), lambda b, t: (b, 0, 0)),
                  _full((D_MODEL, D_MODEL)), _full((1, D_MODEL))],
        out_specs=pl.BlockSpec((1, TB, D_MODEL), lambda b, t: (b, t, 0)),
        out_shape=jax.ShapeDtypeStruct((B, T, D_MODEL), x.dtype),
        scratch_shapes=[pltpu.VMEM((D_QK, GLA_DV), F32),
                        pltpu.VMEM((D_QK, D_GLA), BF16),
                        pltpu.VMEM((TB + POOL_HIST, D_POOL), F32),
                        pltpu.VMEM((TB, D_MODEL), BF16)],
        compiler_params=pltpu.CompilerParams(dimension_semantics=("arbitrary", "arbitrary"),
                                             vmem_limit_bytes=VMEM_LIMIT_BYTES),
        name="hybrid_block",
    )(x, w_main, w_gklr, w_gkup, row(b_gk), row(norm_g), row(gla_norm_g), pool_w_bd,
      row(pool_scale), kbdt, vbd, w_out.astype(BF16), row(final_g))
```

```python
import functools

import jax
import jax.numpy as jnp
from jax import lax
from jax.experimental import pallas as pl
from jax.experimental.pallas import tpu as pltpu

D_MODEL = 1024
N_MEM = 256
D_GLA = 512
GLA_HEADS = 4
GLA_DK = 64
GLA_DV = 128
D_QK = GLA_HEADS * GLA_DK
GLA_RANK = 16
GLA_NORMALIZER = 16.0
D_POOL = 256
POOL_WINDOWS = (2, 4, 8, 16)
POOL_GROUP = 64
POOL_HIST = max(POOL_WINDOWS)
D_MEM = 256
MEM_HEADS = 4
MEM_DH = 64
EPS = 1e-6

LANES = 128
TB = 512
CHUNK = 128
VMEM_LIMIT_BYTES = 56 * 1024 * 1024

C_Q, C_K, C_V, C_GG, C_U, C_GP, C_QM, C_GM = 0, 256, 512, 1024, 1536, 1792, 2048, 2304
D_IN_MAIN = 2560

F32 = jnp.float32
BF16 = jnp.bfloat16


def _dot(a, b):
    return jnp.dot(a, b, preferred_element_type=F32)


def _dot_nt(a, b):
    return lax.dot_general(a, b, (((1,), (1,)), ((), ())), preferred_element_type=F32)


def _rms(x, g):
    return x * lax.rsqrt(jnp.mean(x * x, axis=-1, keepdims=True) + EPS) * g


def _silu(x):
    return x / (1.0 + jnp.exp(-x))


def _mem_kv_kernel(mem_ref, g_ref, w_ref, kbdt_ref, vbd_ref):
    mem_n = _rms(mem_ref[0], g_ref[...]).astype(BF16)
    kv = _dot(mem_n, w_ref[...])
    k_m = kv[:, :D_MEM] * (MEM_DH ** -0.5)
    v_m = kv[:, D_MEM:]
    lane = lax.broadcasted_iota(jnp.int32, (N_MEM, D_MEM), 1)
    for h in range(MEM_HEADS):
        in_head = (lane >= h * MEM_DH) & (lane < (h + 1) * MEM_DH)
        rows = slice(h * N_MEM, (h + 1) * N_MEM)
        kbdt_ref[0, rows, :] = jnp.where(in_head, k_m, 0.0).astype(BF16)
        vbd_ref[0, rows, :] = jnp.where(in_head, v_m, 0.0).astype(BF16)


def _norm_in(x_ref, norm_g_ref, h_ref):
    h_ref[...] = _rms(x_ref[0], norm_g_ref[...]).astype(BF16)


def _project(h_ref, w_in_ref, w_gklr_ref, w_gkup_ref, b_gk_ref, proj):
    qk_ref, v_ref, gg_ref, logit_ref, up_ref, qm_ref = proj
    qk_ref[...] = _dot(h_ref[...], w_in_ref[:, C_Q:C_V])
    gk_lr = _dot(h_ref[...], w_gklr_ref[...]).astype(BF16)
    logit_ref[...] = _dot(gk_lr, w_gkup_ref[...]) + b_gk_ref[...]
    v_ref[...] = _dot(h_ref[...], w_in_ref[:, C_V:C_GG]).astype(BF16)
    gg_ref[...] = _dot(h_ref[...], w_in_ref[:, C_GG:C_U])
    up_ref[...] = _dot(h_ref[...], w_in_ref[:, C_U:C_QM])
    qm_ref[...] = _dot(h_ref[...], w_in_ref[:, C_QM:D_IN_MAIN])


def _state_operand(s_val, p):
    cols = []
    for hd in (2 * p, 2 * p + 1):
        blk = s_val[hd * GLA_DK:(hd + 1) * GLA_DK, :].astype(BF16)
        parts = []
        if hd > 0:
            parts.append(jnp.zeros((hd * GLA_DK, GLA_DV), BF16))
        parts.append(blk)
        if hd < GLA_HEADS - 1:
            parts.append(jnp.zeros(((GLA_HEADS - 1 - hd) * GLA_DK, GLA_DV), BF16))
        cols.append(jnp.concatenate(parts, axis=0))
    return jnp.concatenate(cols, axis=1)


def _mix(t, proj, x_ref, gla_g_ref, pool_w_ref, pool_scale_ref, kbdt_ref, vbd_ref, w_out_ref,
         final_g_ref, o_ref, s_ref, uext_ref, mixed_ref):
    qk_ref, v_ref, gg_ref, logit_ref, up_ref, qm_ref = proj

    row_c = lax.broadcasted_iota(jnp.int32, (CHUNK, 2 * CHUNK), 0)
    col_c = lax.broadcasted_iota(jnp.int32, (CHUNK, 2 * CHUNK), 1)
    causal2 = (col_c & (CHUNK - 1)) <= row_c
    tril2 = jnp.where(causal2, 1.0, 0.0).astype(BF16)
    lane_p = lax.broadcasted_iota(jnp.int32, (CHUNK, LANES), 1)
    lo_half = lane_p < GLA_DK
    zeros_v = jnp.zeros((CHUNK, GLA_DV), BF16)
    gla_g = gla_g_ref[...]
    s_val = s_ref[...]

    for c in range(TB // CHUNK):
        r = slice(c * CHUNK, (c + 1) * CHUNK)
        logit = logit_ref[r, :]
        gk_c = -(jnp.maximum(-logit, 0.0) + jnp.log(1.0 + jnp.exp(-jnp.abs(logit)))) * (1.0 / GLA_NORMALIZER)
        gk_hi = gk_c.astype(BF16)
        gk_lo = (gk_c - gk_hi.astype(F32)).astype(BF16)
        b = _dot(tril2, jnp.concatenate([gk_hi, gk_lo], axis=0))
        b_mid = b[CHUNK // 2 - 1:CHUNK // 2, :]
        b_last = b[CHUNK - 1:CHUNK, :]
        q_c = qk_ref[r, 0:D_QK] * (GLA_DK ** -0.5)
        k_c = qk_ref[r, D_QK:2 * D_QK]
        q_in = q_c * jnp.exp(b - b_mid)
        k_in = k_c * jnp.exp(b_mid - b)
        q_dec = (q_c * jnp.exp(b)).astype(BF16)
        k_end = k_c * jnp.exp(b_last - b)
        v_c = v_ref[r, :]

        o_pairs = []
        for p in range(GLA_HEADS // 2):
            lanes = slice(p * LANES, (p + 1) * LANES)
            q_p = q_in[:, lanes].astype(BF16)
            k_p = k_in[:, lanes]
            wt = jnp.concatenate([jnp.where(lo_half, k_p, 0.0), jnp.where(lo_half, 0.0, k_p)],
                                 axis=0).astype(BF16)
            sc = jnp.where(causal2, _dot_nt(q_p, wt), 0.0).astype(BF16)
            va = v_c[:, (2 * p) * GLA_DV:(2 * p + 1) * GLA_DV]
            vb = v_c[:, (2 * p + 1) * GLA_DV:(2 * p + 2) * GLA_DV]
            vbd = jnp.concatenate([jnp.concatenate([va, zeros_v], axis=1),
                                   jnp.concatenate([zeros_v, vb], axis=1)], axis=0)
            lhs = jnp.concatenate([sc, q_dec], axis=1)
            rhs = jnp.concatenate([vbd, _state_operand(s_val, p)], axis=0)
            o_pairs.append(_dot(lhs, rhs))

        upd = _dot(k_end.T.astype(BF16), v_c)
        dec_t = jnp.broadcast_to(jnp.exp(b_last), (CHUNK, D_QK)).T
        s_val = s_val * dec_t + jnp.concatenate(
            [upd[hd * GLA_DK:(hd + 1) * GLA_DK, hd * GLA_DV:(hd + 1) * GLA_DV] for hd in range(GLA_HEADS)],
            axis=0)

        o = jnp.concatenate(o_pairs, axis=1)
        o_n = jnp.concatenate(
            [_rms(o[:, hd * GLA_DV:(hd + 1) * GLA_DV], gla_g) for hd in range(GLA_HEADS)], axis=1)
        mixed_ref[r, 0:D_GLA] = (o_n * _silu(gg_ref[r, :])).astype(BF16)

    s_ref[...] = s_val

    u = up_ref[:, :D_POOL]
    uext_ref[POOL_HIST:POOL_HIST + TB, :] = u

    def back(j, lanes):
        return uext_ref[POOL_HIST - j:POOL_HIST - j + TB, lanes]

    def window_sum(j0, n, lanes):
        if n == 1:
            return back(j0, lanes)
        return window_sum(j0, n // 2, lanes) + window_sum(j0 + n // 2, n // 2, lanes)

    pos = t * TB + lax.broadcasted_iota(jnp.int32, (TB, LANES), 0)
    lane_t = lax.broadcasted_iota(jnp.int32, (TB, LANES), 1)
    first_group = lane_t < POOL_GROUP
    pooled = []
    for tile in range(D_POOL // LANES):
        lanes = slice(tile * LANES, (tile + 1) * LANES)
        w_a, w_b = POOL_WINDOWS[2 * tile], POOL_WINDOWS[2 * tile + 1]
        s_a = window_sum(0, w_a, lanes)
        s_b = s_a + window_sum(w_a, w_b - w_a, lanes)
        width = jnp.where(first_group, w_a, w_b)
        count = jnp.minimum(pos + 1, width).astype(F32)
        pooled.append(jnp.where(first_group, s_a, s_b) / count - u[:, lanes])
    pooled = jnp.concatenate(pooled, axis=1).astype(BF16)
    o_pool = _dot(pooled, pool_w_ref[...]) * pool_scale_ref[...] * _silu(up_ref[:, D_POOL:])
    mixed_ref[:, D_GLA:D_GLA + D_POOL] = o_pool.astype(BF16)
    uext_ref[0:POOL_HIST, :] = uext_ref[TB:TB + POOL_HIST, :]

    s = _dot_nt(qm_ref[:, :D_MEM].astype(BF16), kbdt_ref[0])
    probs = []
    for hd in range(MEM_HEADS):
        s_h = s[:, hd * N_MEM:(hd + 1) * N_MEM]
        e = jnp.exp(s_h - jnp.max(s_h, axis=-1, keepdims=True))
        probs.append((e / jnp.sum(e, axis=-1, keepdims=True)).astype(BF16))
    o_mem = _dot(jnp.concatenate(probs, axis=1), vbd_ref[0])
    mixed_ref[:, D_GLA + D_POOL:] = (o_mem * _silu(qm_ref[:, D_MEM:])).astype(BF16)

    y = x_ref[0] + _dot(mixed_ref[...], w_out_ref[...])
    o_ref[0] = _rms(y, final_g_ref[...])


def _block_kernel(nt, x_ref, xn_ref, w_in_ref, w_gklr_ref, w_gkup_ref, b_gk_ref, norm_g_ref,
                  gla_g_ref, pool_w_ref, pool_scale_ref, kbdt_ref, vbd_ref, w_out_ref, final_g_ref,
                  o_ref, s_ref, uext_ref, mixed_ref, h0_ref, h1_ref, *proj_refs):
    s = pl.program_id(0)
    g = s - 1
    t = g % nt
    h = (h0_ref, h1_ref)
    proj = (proj_refs[:6], proj_refs[6:])
    w_args = (w_in_ref, w_gklr_ref, w_gkup_ref, b_gk_ref)
    mix_args = (x_ref, gla_g_ref, pool_w_ref, pool_scale_ref, kbdt_ref, vbd_ref, w_out_ref,
                final_g_ref, o_ref, s_ref, uext_ref, mixed_ref)

    @pl.when(s == 0)
    def _prime():
        _norm_in(x_ref, norm_g_ref, h[0])
        _project(h[0], *w_args, proj[0])
        _norm_in(xn_ref, norm_g_ref, h[1])

    @pl.when((s > 0) & (t == 0))
    def _reset():
        s_ref[...] = jnp.zeros_like(s_ref)
        uext_ref[0:POOL_HIST, :] = jnp.zeros((POOL_HIST, D_POOL), F32)

    for parity in range(2):
        @pl.when((s > 0) & (g % 2 == parity))
        def _step():
            _mix(t, proj[parity], *mix_args)
            _project(h[1 - parity], *w_args, proj[1 - parity])
            _norm_in(xn_ref, norm_g_ref, h[parity])


def _full(shape):
    return pl.BlockSpec(shape, lambda *_: (0,) * len(shape))


@jax.jit
def kernel(x, mem, norm_g, w_in, w_gk_up, b_gk, gla_norm_g, pool_w, pool_scale, mem_norm_g,
           w_mem_kv, w_out, final_g):
    B, T, D = x.shape
    assert D == D_MODEL and T % TB == 0 and mem.shape == (B, N_MEM, D_MODEL)
    nt = T // TB
    n_blocks = B * nt
    assert nt % 2 == 0

    gk0 = C_GG
    w_main = jnp.concatenate([w_in[:, :gk0], w_in[:, gk0 + GLA_RANK:]], axis=1).astype(BF16)
    w_gklr = jnp.pad(w_in[:, gk0:gk0 + GLA_RANK], ((0, 0), (0, LANES - GLA_RANK))).astype(BF16)
    w_gkup = jnp.pad(w_gk_up, ((0, LANES - GLA_RANK), (0, 0))).astype(BF16)
    pool_w_bd = jax.scipy.linalg.block_diag(*[pool_w[g] for g in range(len(POOL_WINDOWS))]).astype(BF16)
    row = lambda a: a.reshape(1, -1).astype(F32)

    kbdt, vbd = pl.pallas_call(
        _mem_kv_kernel,
        grid=(B,),
        in_specs=[pl.BlockSpec((1, N_MEM, D_MODEL), lambda b: (b, 0, 0)),
                  _full((1, D_MODEL)), _full((D_MODEL, 2 * D_MEM))],
        out_specs=[pl.BlockSpec((1, MEM_HEADS * N_MEM, D_MEM), lambda b: (b, 0, 0))] * 2,
        out_shape=[jax.ShapeDtypeStruct((B, MEM_HEADS * N_MEM, D_MEM), BF16)] * 2,
        name="mem_kv",
    )(mem, row(mem_norm_g), w_mem_kv.astype(BF16))

    def blk(g):
        g = jnp.clip(g, 0, n_blocks - 1)
        return (g // nt, g % nt, 0)

    proj_slot = [pltpu.VMEM((TB, 2 * D_QK), F32), pltpu.VMEM((TB, D_GLA), BF16),
                 pltpu.VMEM((TB, D_GLA), F32), pltpu.VMEM((TB, D_QK), F32),
                 pltpu.VMEM((TB, 2 * D_POOL), F32), pltpu.VMEM((TB, 2 * D_MEM), F32)]
    mem_spec = pl.BlockSpec((1, MEM_HEADS * N_MEM, D_MEM), lambda s: (jnp.maximum(s - 1, 0) // nt, 0, 0))
    return pl.pallas_call(
        functools.partial(_block_kernel, nt),
        grid=(n_blocks + 1,),
        in_specs=[pl.BlockSpec((1, TB, D_MODEL), lambda s: blk(s - 1)),
                  pl.BlockSpec((1, TB, D_MODEL), lambda s: blk(s + 1)),
                  _full((D_MODEL, D_IN_MAIN)), _full((D_MODEL, LANES)), _full((LANES, D_QK)),
                  _full((1, D_QK)), _full((1, D_MODEL)), _full((1, GLA_DV)),
                  _full((D_POOL, D_POOL)), _full((1, D_POOL)), mem_spec, mem_spec,
                  _full((D_MODEL, D_MODEL)), _full((1, D_MODEL))],
        out_specs=pl.BlockSpec((1, TB, D_MODEL), lambda s: blk(s - 1)),
        out_shape=jax.ShapeDtypeStruct((B, T, D_MODEL), x.dtype),
        scratch_shapes=[pltpu.VMEM((D_QK, GLA_DV), F32),
                        pltpu.VMEM((TB + POOL_HIST, D_POOL), F32),
                        pltpu.VMEM((TB, D_MODEL), BF16),
                        pltpu.VMEM((TB, D_MODEL), BF16),
                        pltpu.VMEM((TB, D_MODEL), BF16),
                        ] + proj_slot + proj_slot,
        compiler_params=pltpu.CompilerParams(dimension_semantics=("arbitrary",),
                                             vmem_limit_bytes=VMEM_LIMIT_BYTES),
        name="hybrid_block",
    )(x, x, w_main, w_gklr, w_gkup, row(b_gk), row(norm_g), row(gla_norm_g), pool_w_bd,
      row(pool_scale), kbdt, vbd, w_out.astype(BF16), row(final_g))
```
